```python
import jax, jax.numpy as jnp
from jax import lax
import numpy as np

D_MODEL = 1024
BATCH = 8
SEQ = 2048
DEPTH = 2

CONV_CH = 512
CONV_K = 3
GLA_HEADS = 4
GLA_DK = 64
GLA_DV = 128
GLA_KW = GLA_HEADS * GLA_DK
GLA_VW = GLA_HEADS * GLA_DV
GLA_RANK = 16
GLA_TAU = 16.0
GLA_CHUNK = 64
N_BRANCH = 2
IN_SIZES = (CONV_CH, CONV_CH, CONV_CH, GLA_KW, GLA_KW, GLA_VW, GLA_VW, GLA_RANK, D_MODEL, D_MODEL)
IN_COLS = 3 * CONV_CH + 2 * GLA_KW + 2 * GLA_VW + GLA_RANK + 2 * D_MODEL
D_FF = 2816
N_EXPERTS = 8
TOP_K = 2
D_FF_EXPERT = 3584
N_DENSE = (DEPTH + 1) // 2
N_MOE = DEPTH // 2
EPS = 1e-6

kernel_name = "hybrid_conv_gla_moe_block"


def rmsnorm(x, g):
    xf = x.astype(jnp.float32)
    y = xf * lax.rsqrt(jnp.mean(xf * xf, axis=-1, keepdims=True) + EPS)
    return (y * g.astype(jnp.float32)).astype(x.dtype)


def short_conv(u, w):
    c = u.shape[-1]
    return lax.conv_general_dilated(u, w.astype(u.dtype)[:, None, :], window_strides=(1,),
                                    padding=[(CONV_K - 1, 0)],
                                    dimension_numbers=("NWC", "WIO", "NWC"),
                                    feature_group_count=c)


def gla(q, k, v, log_a):
    bn, t, h, dk = q.shape
    dv = v.shape[-1]
    L = GLA_CHUNK
    n = t // L

    def to_chunks(z):
        return z.astype(jnp.float32).reshape(bn, n, L, h, z.shape[-1]).transpose(1, 0, 3, 2, 4)

    qc, kc, vc, gc = (to_chunks(z) for z in (q * (dk ** -0.5), k, v, log_a))
    causal = jnp.tril(jnp.ones((L, L), dtype=bool))

    def step(S, inp):
        q_, k_, v_, g_ = inp
        b = jnp.cumsum(g_, axis=-2)
        o_inter = jnp.einsum("bhlk,bhkv->bhlv", q_ * jnp.exp(b), S)
        diff = b[:, :, :, None, :] - b[:, :, None, :, :]
        decay = jnp.exp(jnp.where(causal[:, :, None], diff, -jnp.inf))
        scores = jnp.einsum("bhik,bhjk,bhijk->bhij", q_, k_, decay)
        o = o_inter + jnp.einsum("bhij,bhjv->bhiv", scores, v_)
        b_last = b[:, :, -1:, :]
        S = jnp.exp(b_last[:, :, 0, :])[..., None] * S + jnp.einsum(
            "bhjk,bhjv->bhkv", k_ * jnp.exp(b_last - b), v_)
        return S, o

    S0 = jnp.zeros((bn, h, dk, dv), jnp.float32)
    _, o = lax.scan(step, S0, (qc, kc, vc, gc))
    return o.transpose(1, 0, 3, 2, 4).reshape(bn, t, h, dv)


def token_mixer(hn, w_in, w_a2, b_a, conv_w, gla_norm_g, w_branch, w_out):
    bn, t, _ = hn.shape
    p = hn @ w_in
    idx, acc = [], 0
    for s in IN_SIZES[:-1]:
        acc += s
        idx.append(acc)
    xc, gb_c, gc_c, q, k, v, g_out, a_low, gate_a, gate_b = jnp.split(p, idx, axis=-1)
    y_a = gb_c * short_conv(gc_c * xc, conv_w)
    log_a = jax.nn.log_sigmoid((a_low @ w_a2 + b_a).astype(jnp.float32)) / GLA_TAU
    o = gla(q.reshape(bn, t, GLA_HEADS, GLA_DK), k.reshape(bn, t, GLA_HEADS, GLA_DK),
            v.reshape(bn, t, GLA_HEADS, GLA_DV), log_a.reshape(bn, t, GLA_HEADS, GLA_DK))
    o = rmsnorm(o, gla_norm_g).astype(hn.dtype).reshape(bn, t, GLA_VW)
    y_b = o * jax.nn.silu(g_out)
    ya_d = y_a @ w_branch[0]
    yb_d = y_b @ w_branch[1]
    merged = jax.nn.sigmoid(gate_a) * ya_d + jax.nn.sigmoid(gate_b) * yb_d
    return merged @ w_out


def swiglu(h, w_gate, w_up, w_down):
    return (jax.nn.silu(h @ w_gate) * (h @ w_up)) @ w_down


def moe(h, w_router, w_gate, w_up, w_down):
    logits = (h @ w_router).astype(jnp.float32)
    top_l, top_i = lax.top_k(logits, TOP_K)
    top_w = jax.nn.softmax(top_l, axis=-1)
    gates = jnp.sum(jax.nn.one_hot(top_i, N_EXPERTS, dtype=jnp.float32) * top_w[..., None],
                    axis=-2).astype(h.dtype)
    y = jnp.zeros_like(h)
    for e in range(N_EXPERTS):
        y = y + gates[..., e:e + 1] * swiglu(h, w_gate[e], w_up[e], w_down[e])
    return y


def setup_inputs(seed: int = 0) -> dict:
    key = jax.random.key(seed)
    ks = jax.random.split(key, 24)
    f32 = jnp.float32

    def nrm(k, shape, fan_in):
        return jax.random.normal(k, shape, f32) * (fan_in ** -0.5)

    def gain(k, shape):
        return 1.0 + 0.02 * jax.random.normal(k, shape, f32)

    return {
        "x": jax.random.normal(ks[0], (BATCH, SEQ, D_MODEL), f32),
        "ln1_g": gain(ks[1], (DEPTH, D_MODEL)),
        "w_in": nrm(ks[2], (DEPTH, D_MODEL, IN_COLS), D_MODEL),
        "w_a2": nrm(ks[3], (DEPTH, GLA_RANK, GLA_KW), GLA_RANK),
        "b_a": 0.1 * jax.random.normal(ks[4], (DEPTH, GLA_KW), f32),
        "conv_w": nrm(ks[5], (DEPTH, CONV_K, CONV_CH), CONV_K),
        "gla_norm_g": gain(ks[6], (DEPTH, GLA_DV)),
        "w_branch": nrm(ks[7], (DEPTH, N_BRANCH, CONV_CH, D_MODEL), CONV_CH),
        "w_out": nrm(ks[8], (DEPTH, D_MODEL, D_MODEL), D_MODEL),
        "ln2_g": gain(ks[9], (DEPTH, D_MODEL)),
        "ffn_w_gate": nrm(ks[10], (N_DENSE, D_MODEL, D_FF), D_MODEL),
        "ffn_w_up": nrm(ks[11], (N_DENSE, D_MODEL, D_FF), D_MODEL),
        "ffn_w_down": nrm(ks[12], (N_DENSE, D_FF, D_MODEL), D_FF),
        "w_router": nrm(ks[13], (N_MOE, D_MODEL, N_EXPERTS), D_MODEL),
        "moe_w_gate": nrm(ks[14], (N_MOE, N_EXPERTS, D_MODEL, D_FF_EXPERT), D_MODEL),
        "moe_w_up": nrm(ks[15], (N_MOE, N_EXPERTS, D_MODEL, D_FF_EXPERT), D_MODEL),
        "moe_w_down": nrm(ks[16], (N_MOE, N_EXPERTS, D_FF_EXPERT, D_MODEL), D_FF_EXPERT),
        "final_g": gain(ks[17], (D_MODEL,)),
    }


def reference(x, ln1_g, w_in, w_a2, b_a, conv_w, gla_norm_g, w_branch, w_out, ln2_g,
              ffn_w_gate, ffn_w_up, ffn_w_down, w_router, moe_w_gate, moe_w_up, moe_w_down,
              final_g):
    for l in range(DEPTH):
        x = x + token_mixer(rmsnorm(x, ln1_g[l]), w_in[l], w_a2[l], b_a[l], conv_w[l],
                            gla_norm_g[l], w_branch[l], w_out[l])
        h = rmsnorm(x, ln2_g[l])
        j = l // 2
        if l % 2 == 0:
            x = x + swiglu(h, ffn_w_gate[j], ffn_w_up[j], ffn_w_down[j])
        else:
            x = x + moe(h, w_router[j], moe_w_gate[j], moe_w_up[j], moe_w_down[j])
    return rmsnorm(x, final_g)
```

```python
import functools

import jax
import jax.numpy as jnp
from jax import lax
from jax.experimental import pallas as pl
from jax.experimental.pallas import tpu as pltpu

F32 = jnp.float32
BF16 = jnp.bfloat16

EPS = 1e-6
CONV_CH = 512
GLA_HEADS = 4
GLA_DK = 64
GLA_DV = 128
GLA_KW = GLA_HEADS * GLA_DK
GLA_VW = GLA_HEADS * GLA_DV
GLA_RANK = 16
GLA_TAU = 16.0
GLA_CHUNK = 64
TOP_K = 2
LANES = 128

C_XC, C_GB, C_GC = 0, CONV_CH, 2 * CONV_CH
C_Q = 3 * CONV_CH
C_K = C_Q + GLA_KW
C_V = C_K + GLA_KW
C_GO = C_V + GLA_VW
C_GA = C_GO + GLA_VW
VMEM_LIMIT = 56 * 1024 * 1024


def _rms(x, g):
    return x * lax.rsqrt(jnp.mean(x * x, axis=-1, keepdims=True) + EPS) * g


def _silu(x):
    return x * jax.nn.sigmoid(x)


def _resident(shape):
    nd = len(shape)
    return pl.BlockSpec(shape, lambda *_: (0,) * nd, pipeline_mode=pl.Buffered(1))


def _inproj_kernel(x_ref, g_ref, w_ref, p_ref, *, col_chunk):
    hn = _rms(x_ref[...], g_ref[...]).astype(BF16)
    n = w_ref.shape[1]
    for c0 in range(0, n, col_chunk):
        c1 = min(n, c0 + col_chunk)
        p_ref[:, c0:c1] = jnp.dot(hn, w_ref[:, c0:c1], preferred_element_type=F32).astype(BF16)


def _inproj(x2, g, w, *, tm=512):
    m, d = x2.shape
    n = w.shape[1]
    return pl.pallas_call(
        functools.partial(_inproj_kernel, col_chunk=1024),
        grid=(m // tm,),
        in_specs=[pl.BlockSpec((tm, d), lambda i: (i, 0)),
                  _resident((1, d)),
                  _resident((d, n))],
        out_specs=pl.BlockSpec((tm, n), lambda i: (i, 0)),
        out_shape=jax.ShapeDtypeStruct((m, n), BF16),
        compiler_params=pltpu.CompilerParams(dimension_semantics=("arbitrary",),
                                             vmem_limit_bytes=VMEM_LIMIT),
        name="inproj",
    )(x2, g, w)


def _mixer_kernel(p_ref, x_ref, wa2_ref, ba_ref, cw_ref, gg_ref, wbr_ref, wout_ref,
                  o_ref, s_ref, tail_ref, *, tt):
    L = GLA_CHUNK
    nc = tt // L

    @pl.when(pl.program_id(1) == 0)
    def _():
        s_ref[...] = jnp.zeros_like(s_ref)
        tail_ref[...] = jnp.zeros_like(tail_ref)

    u = p_ref[:, C_GC:C_GC + CONV_CH].astype(F32) * p_ref[:, C_XC:C_XC + CONV_CH].astype(F32)
    row = lax.broadcasted_iota(jnp.int32, (tt, CONV_CH), 0)
    prev1 = tail_ref[7:8, :]
    prev2 = tail_ref[6:7, :]
    u1 = jnp.where(row == 0, prev1, pltpu.roll(u, 1, 0))
    u2 = jnp.where(row == 0, prev2, jnp.where(row == 1, prev1, pltpu.roll(u, 2, 0)))
    tail_ref[...] = u[tt - 8:tt, :]
    cw = cw_ref[...]
    y_a = p_ref[:, C_GB:C_GB + CONV_CH].astype(F32) * (cw[2:3] * u + cw[1:2] * u1 + cw[0:1] * u2)

    z = jnp.dot(p_ref[:, C_GA + 2 * 1024:C_GA + 2 * 1024 + LANES], wa2_ref[...],
                preferred_element_type=F32) + ba_ref[...]
    g = (jnp.minimum(z, 0.0) - jnp.log1p(jnp.exp(-jnp.abs(z)))) * (1.0 / GLA_TAU)
    g_hi = g.astype(BF16)
    g_lo = (g - g_hi.astype(F32)).astype(BF16)
    r_i = lax.broadcasted_iota(jnp.int32, (tt, tt), 0)
    c_i = lax.broadcasted_iota(jnp.int32, (tt, tt), 1)
    causal = (r_i >= c_i) & ((r_i // L) == (c_i // L))
    tri = causal.astype(BF16)
    b = (jnp.dot(tri, g_hi, preferred_element_type=F32)
         + jnp.dot(tri, g_lo, preferred_element_type=F32))
    b3 = b.reshape(nc, L, GLA_KW)
    b_mid = b3[:, L // 2 - 1:L // 2, :]
    b_last = b3[:, L - 1:L, :]
    q3 = p_ref[:, C_Q:C_Q + GLA_KW].astype(F32).reshape(nc, L, GLA_KW) * (GLA_DK ** -0.5)
    k3 = p_ref[:, C_K:C_K + GLA_KW].astype(F32).reshape(nc, L, GLA_KW)
    qe = (q3 * jnp.exp(b3 - b_mid)).reshape(tt, GLA_KW)
    ke = (k3 * jnp.exp(b_mid - b3)).reshape(tt, GLA_KW).astype(BF16)
    qs = (q3 * jnp.exp(b3)).astype(BF16)
    kl = (k3 * jnp.exp(b_last - b3)).astype(BF16)
    v = p_ref[:, C_V:C_V + GLA_VW]

    lane_head = lax.broadcasted_iota(jnp.int32, (tt, GLA_KW), 1) // GLA_DK
    o_heads = []
    for h in range(GLA_HEADS):
        qe_h = jnp.where(lane_head == h, qe, 0.0).astype(BF16)
        a_h = lax.dot_general(qe_h, ke, (((1,), (1,)), ((), ())), preferred_element_type=F32)
        a_h = jnp.where(causal, a_h, 0.0).astype(BF16)
        o_heads.append(jnp.dot(a_h, v[:, h * GLA_DV:(h + 1) * GLA_DV], preferred_element_type=F32))

    stack_head = lax.broadcasted_iota(jnp.int32, (GLA_HEADS * L, GLA_KW), 0) // L
    stack_lane_head = lax.broadcasted_iota(jnp.int32, (GLA_HEADS * L, GLA_KW), 1) // GLA_DK
    ones_cols = jnp.ones((L, GLA_DV), BF16)
    s = s_ref[...]
    inter = []
    for c in range(nc):
        qs_c = jnp.concatenate([qs[c]] * GLA_HEADS, axis=0)
        qs_c = jnp.where(stack_head == stack_lane_head, qs_c, jnp.zeros_like(qs_c))
        inter.append(jnp.dot(qs_c, s.astype(BF16), preferred_element_type=F32))
        v_c = v[c * L:(c + 1) * L, :]
        r_c = lax.dot_general(kl[c], v_c, (((0,), (0,)), ((), ())), preferred_element_type=F32)
        kv = jnp.concatenate([r_c[h * GLA_DK:(h + 1) * GLA_DK, h * GLA_DV:(h + 1) * GLA_DV]
                              for h in range(GLA_HEADS)], axis=0)
        dsum = (lax.dot_general(g_hi[c * L:(c + 1) * L, :], ones_cols, (((0,), (0,)), ((), ())),
                                preferred_element_type=F32)
                + lax.dot_general(g_lo[c * L:(c + 1) * L, :], ones_cols, (((0,), (0,)), ((), ())),
                                  preferred_element_type=F32))
        s = jnp.exp(dsum) * s + kv
    s_ref[...] = s

    gg = gg_ref[...]
    yb = []
    for h in range(GLA_HEADS):
        o_h = o_heads[h] + jnp.concatenate([inter[c][h * L:(h + 1) * L, :] for c in range(nc)], axis=0)
        o_h = _rms(o_h, gg)
        go = p_ref[:, C_GO + h * GLA_DV:C_GO + (h + 1) * GLA_DV].astype(F32)
        yb.append((o_h * _silu(go)).astype(BF16))
    y_b = jnp.concatenate(yb, axis=1)

    ya_d = jnp.dot(y_a.astype(BF16), wbr_ref[0], preferred_element_type=F32)
    yb_d = jnp.dot(y_b, wbr_ref[1], preferred_element_type=F32)
    d = ya_d.shape[1]
    merged = (jax.nn.sigmoid(p_ref[:, C_GA:C_GA + d].astype(F32)) * ya_d
              + jax.nn.sigmoid(p_ref[:, C_GA + d:C_GA + 2 * d].astype(F32)) * yb_d)
    o_ref[...] = x_ref[...] + jnp.dot(merged.astype(BF16), wout_ref[...], preferred_element_type=F32)


def _mixer(p, x2, wa2p, ba, cw, gg, wbr, wout, *, batch, seq, tt=256):
    m, d = x2.shape
    pc = p.shape[1]
    nt = seq // tt
    return pl.pallas_call(
        functools.partial(_mixer_kernel, tt=tt),
        grid=(batch, nt),
        in_specs=[pl.BlockSpec((tt, pc), lambda b, t: (b * nt + t, 0)),
                  pl.BlockSpec((tt, d), lambda b, t: (b * nt + t, 0)),
                  _resident(wa2p.shape), _resident(ba.shape), _resident(cw.shape),
                  _resident(gg.shape), _resident(wbr.shape), _resident(wout.shape)],
        out_specs=pl.BlockSpec((tt, d), lambda b, t: (b * nt + t, 0)),
        out_shape=jax.ShapeDtypeStruct((m, d), F32),
        scratch_shapes=[pltpu.VMEM((GLA_KW, GLA_DV), F32),
                        pltpu.VMEM((8, CONV_CH), F32)],
        compiler_params=pltpu.CompilerParams(dimension_semantics=("arbitrary", "arbitrary"),
                                             vmem_limit_bytes=VMEM_LIMIT),
        name="mixer",
    )(p, x2, wa2p, ba, cw, gg, wbr, wout)


def _ffn_kernel(x_ref, g_ref, wg_ref, wu_ref, wd_ref, o_ref, *, ff_chunk):
    x = x_ref[...]
    h = _rms(x, g_ref[...]).astype(BF16)
    acc = x
    for c0 in range(0, wg_ref.shape[1], ff_chunk):
        a = (_silu(jnp.dot(h, wg_ref[:, c0:c0 + ff_chunk], preferred_element_type=F32))
             * jnp.dot(h, wu_ref[:, c0:c0 + ff_chunk], preferred_element_type=F32))
        acc = acc + jnp.dot(a.astype(BF16), wd_ref[c0:c0 + ff_chunk, :], preferred_element_type=F32)
    o_ref[...] = acc


def _ffn(x2, g, wg, wu, wd, *, tm=512):
    m, d = x2.shape
    ff = wg.shape[1]
    return pl.pallas_call(
        functools.partial(_ffn_kernel, ff_chunk=ff // 2),
        grid=(m // tm,),
        in_specs=[pl.BlockSpec((tm, d), lambda i: (i, 0)),
                  _resident((1, d)), _resident(wg.shape), _resident(wu.shape), _resident(wd.shape)],
        out_specs=pl.BlockSpec((tm, d), lambda i: (i, 0)),
        out_shape=jax.ShapeDtypeStruct((m, d), F32),
        compiler_params=pltpu.CompilerParams(dimension_semantics=("arbitrary",),
                                             vmem_limit_bytes=VMEM_LIMIT),
        name="ffn",
    )(x2, g, wg, wu, wd)


def _router_kernel(x_ref, g_ref, wr_ref, idx_ref, wgt_ref):
    h = _rms(x_ref[...], g_ref[...])
    lg = lax.dot_general(wr_ref[...], h, (((1,), (1,)), ((), ())),
                         precision=lax.Precision.HIGHEST, preferred_element_type=F32)
    ne = lg.shape[0]
    eid = lax.broadcasted_iota(jnp.int32, lg.shape, 0)
    m1 = jnp.max(lg, axis=0, keepdims=True)
    i1 = jnp.min(jnp.where(lg == m1, eid, ne), axis=0, keepdims=True)
    lg2 = jnp.where(eid == i1, -jnp.inf, lg)
    m2 = jnp.max(lg2, axis=0, keepdims=True)
    i2 = jnp.min(jnp.where(lg2 == m2, eid, ne), axis=0, keepdims=True)
    e2 = jnp.exp(m2 - m1)
    w1 = 1.0 / (1.0 + e2)
    idx_ref[...] = jnp.concatenate([i1, i2], axis=0)
    wgt_ref[...] = jnp.concatenate([w1, e2 * w1], axis=0)


def _router(x2, g, wr_t, *, tm=512):
    m, d = x2.shape
    ne = wr_t.shape[0]
    return pl.pallas_call(
        _router_kernel,
        grid=(m // tm,),
        in_specs=[pl.BlockSpec((tm, d), lambda i: (i, 0)), _resident((1, d)), _resident((ne, d))],
        out_specs=[pl.BlockSpec((TOP_K, tm), lambda i: (0, i)),
                   pl.BlockSpec((TOP_K, tm), lambda i: (0, i))],
        out_shape=[jax.ShapeDtypeStruct((TOP_K, m), jnp.int32),
                   jax.ShapeDtypeStruct((TOP_K, m), F32)],
        compiler_params=pltpu.CompilerParams(dimension_semantics=("arbitrary",)),
        name="router",
    )(x2, g, wr_t)


def _moe_kernel(te_ref, nv_ref, nr_ref, src_ref, dst_ref,
                x_hbm, g_ref, wg_ref, wu_ref, wd_ref, y_hbm,
                rows_ref, xs_ref, acc_ref, sem_in, sem_out, *, tm):
    i = pl.program_id(0)
    j = pl.program_id(1)
    nj = pl.num_programs(1)

    @pl.when(i < nv_ref[0])
    def _():
        @pl.when(j == 0)
        def _():
            def issue(r, carry):
                pltpu.make_async_copy(x_hbm.at[pl.ds(src_ref[i * tm + r], 1), :],
                                      rows_ref.at[pl.ds(r, 1), :], sem_in).start()
                return carry
            lax.fori_loop(0, tm, issue, 0)
            pltpu.make_async_copy(x_hbm.at[pl.ds(0, tm), :], rows_ref, sem_in).wait()
            xs_ref[...] = _rms(rows_ref[...], g_ref[...]).astype(BF16)
            acc_ref[...] = jnp.zeros_like(acc_ref)

        xs = xs_ref[...]
        a = (_silu(jnp.dot(xs, wg_ref[0], preferred_element_type=F32))
             * jnp.dot(xs, wu_ref[0], preferred_element_type=F32))
        acc_ref[...] += jnp.dot(a.astype(BF16), wd_ref[0], preferred_element_type=F32)

        @pl.when(j == nj - 1)
        def _():
            def issue(r, carry):
                pltpu.make_async_copy(acc_ref.at[pl.ds(r, 1), :],
                                      y_hbm.at[pl.ds(dst_ref[i * tm + r], 1), :], sem_out).start()
                return carry
            nr = nr_ref[i]
            lax.fori_loop(0, nr, issue, 0)

            def drain(r, carry):
                pltpu.make_async_copy(acc_ref.at[pl.ds(0, 1), :], y_hbm.at[pl.ds(0, 1), :], sem_out).wait()
                return carry
            lax.fori_loop(0, nr, drain, 0)


def _moe(x2, g, wg, wu, wd, routing, *, tm, tn, n_out_rows):
    m, d = x2.shape
    ne, _, ff = wg.shape
    n_tiles = routing[0].shape[0]
    grid_spec = pltpu.PrefetchScalarGridSpec(
        num_scalar_prefetch=len(routing),
        grid=(n_tiles, ff // tn),
        in_specs=[pl.BlockSpec(memory_space=pl.ANY),
                  pl.BlockSpec((1, d), lambda i, j, te, *_: (0, 0)),
                  pl.BlockSpec((1, d, tn), lambda i, j, te, *_: (te[i], 0, j)),
                  pl.BlockSpec((1, d, tn), lambda i, j, te, *_: (te[i], 0, j)),
                  pl.BlockSpec((1, tn, d), lambda i, j, te, *_: (te[i], j, 0))],
        out_specs=pl.BlockSpec(memory_space=pl.ANY),
        scratch_shapes=[pltpu.VMEM((tm, d), F32),
                        pltpu.VMEM((tm, d), BF16),
                        pltpu.VMEM((tm, d), F32),
                        pltpu.SemaphoreType.DMA(()),
                        pltpu.SemaphoreType.DMA(())],
    )
    return pl.pallas_call(
        functools.partial(_moe_kernel, tm=tm),
        grid_spec=grid_spec,
        out_shape=jax.ShapeDtypeStruct((n_out_rows, d), F32),
        compiler_params=pltpu.CompilerParams(dimension_semantics=("arbitrary", "arbitrary"),
                                             vmem_limit_bytes=VMEM_LIMIT),
        name="moe",
    )(*routing, x2, g, wg, wu, wd)


def _route(top_i, ne, tm):
    k, m = top_i.shape
    na = k * m
    n_tiles = na // tm + ne
    n_slots = n_tiles * tm
    e_flat = top_i.reshape(na)
    onehot = (e_flat[:, None] == jnp.arange(ne, dtype=jnp.int32)[None, :]).astype(jnp.int32)
    csum = jnp.cumsum(onehot, axis=0)
    counts = csum[-1]
    rank = jnp.sum(onehot * csum, axis=1) - 1
    padded = ((counts + tm - 1) // tm) * tm
    ends = jnp.cumsum(padded)
    starts = ends - padded
    slot = starts[e_flat] + rank
    a_ids = jnp.arange(na, dtype=jnp.int32)
    src_row = jnp.zeros((n_slots,), jnp.int32).at[slot].set(a_ids % m)
    dst_row = jnp.zeros((n_slots,), jnp.int32).at[slot].set(a_ids)
    tile_start = jnp.arange(n_tiles, dtype=jnp.int32) * tm
    tile_expert = jnp.minimum(jnp.sum((tile_start[:, None] >= ends[None, :]).astype(jnp.int32), axis=1),
                              ne - 1).astype(jnp.int32)
    tile_rows = jnp.clip(starts[tile_expert] + counts[tile_expert] - tile_start, 0, tm).astype(jnp.int32)
    n_valid = (ends[-1] // tm).astype(jnp.int32).reshape(1)
    return tile_expert, n_valid, tile_rows, src_row, dst_row


def _combine_kernel(x_ref, y0_ref, y1_ref, w_ref, g_ref, o_ref):
    w = w_ref[...]
    o_ref[...] = _rms(x_ref[...] + w[:, 0:1] * y0_ref[...] + w[:, 1:2] * y1_ref[...], g_ref[...])


def _combine(x2, y, w_tok, g, *, tm=512):
    m, d = x2.shape
    nb = m // tm
    return pl.pallas_call(
        _combine_kernel,
        grid=(nb,),
        in_specs=[pl.BlockSpec((tm, d), lambda i: (i, 0)),
                  pl.BlockSpec((tm, d), lambda i: (i, 0)),
                  pl.BlockSpec((tm, d), lambda i: (i + nb, 0)),
                  pl.BlockSpec((tm, TOP_K), lambda i: (i, 0)),
                  _resident((1, d))],
        out_specs=pl.BlockSpec((tm, d), lambda i: (i, 0)),
        out_shape=jax.ShapeDtypeStruct((m, d), F32),
        compiler_params=pltpu.CompilerParams(dimension_semantics=("arbitrary",)),
        name="combine",
    )(x2, y, y, w_tok, g)


def _prep_w_in(w):
    a0 = C_GO + GLA_VW
    pad = jnp.zeros((w.shape[0], LANES - GLA_RANK), w.dtype)
    return jnp.concatenate([w[:, :a0], w[:, a0 + GLA_RANK:], w[:, a0:a0 + GLA_RANK], pad], axis=1).astype(BF16)


def kernel(x, ln1_g, w_in, w_a2, b_a, conv_w, gla_norm_g, w_branch, w_out, ln2_g,
           ffn_w_gate, ffn_w_up, ffn_w_down, w_router, moe_w_gate, moe_w_up, moe_w_down, final_g):
    batch, seq, d = x.shape
    depth = w_in.shape[0]
    m = batch * seq
    x2 = x.reshape(m, d)
    tm_moe = 512
    out = None
    for l in range(depth):
        p = _inproj(x2, ln1_g[l].reshape(1, d), _prep_w_in(w_in[l]))
        wa2p = jnp.concatenate([w_a2[l], jnp.zeros((LANES - GLA_RANK, GLA_KW), F32)], axis=0).astype(BF16)
        x2 = _mixer(p, x2, wa2p, b_a[l].reshape(1, GLA_KW), conv_w[l], gla_norm_g[l].reshape(1, GLA_DV),
                    w_branch[l].astype(BF16), w_out[l].astype(BF16), batch=batch, seq=seq)
        j = l // 2
        g2 = ln2_g[l].reshape(1, d)
        if l % 2 == 0:
            x2 = _ffn(x2, g2, ffn_w_gate[j].astype(BF16), ffn_w_up[j].astype(BF16),
                      ffn_w_down[j].astype(BF16))
        else:
            ne = w_router.shape[-1]
            top_i, top_w = _router(x2, g2, w_router[j].T)
            y = _moe(x2, g2, moe_w_gate[j].astype(BF16), moe_w_up[j].astype(BF16),
                     moe_w_down[j].astype(BF16), _route(top_i, ne, tm_moe),
                     tm=tm_moe, tn=512, n_out_rows=TOP_K * m)
            if l == depth - 1:
                out = _combine(x2, y, top_w.T, final_g.reshape(1, d))
            else:
                raise NotImplementedError("routed layer is only supported as the last layer")
    if out is None:
        raise NotImplementedError("the last layer must be a routed layer")
    return out.reshape(batch, seq, d)
```
